```python
import math
import jax
import jax.numpy as jnp
from jax import lax
import numpy as np

D_MODEL = 2048
BATCH = 2
SEQ = 4096
DEPTH = 4
DEC_BATCH = 8
DEC_SEQ = 4
PAST_LEN = 16384
PAGE_SIZE = 128

HD = 128
H_A = D_MODEL // (2 * HD)
H_B = D_MODEL // (2 * HD)
DQK_A = HD // 2
H_C = D_MODEL // HD
G_C = 4
R_C = H_C // G_C
L_CMP = 32
S_CMP = 16
L_SEL = 64
N_TOP = 16
WINDOW = 512
CMP_HID = HD
D_FF = -(-(8 * D_MODEL) // (3 * 256)) * 256
N_EVEN = (DEPTH + 1) // 2
N_ODD = DEPTH // 2
W_EVEN = 3 * H_A * HD + 3 * H_B * HD + H_B
W_ODD = H_C * HD + 6 * G_C * HD + 3 * H_C
Q_BLOCK = 128
NSA_QB = 64
EPS = 1e-6
NEG = -1e30
FORCE = 1e4
SCALE_A = DQK_A ** -0.5
SCALE_B = HD ** -0.5

kernel_name = 'hybrid_diff_fox_nsa_decode_step'


def rmsnorm(x, g):
    xf = x.astype(jnp.float32)
    y = xf * lax.rsqrt(jnp.mean(xf * xf, axis=-1, keepdims=True) + EPS)
    return (y * g.astype(jnp.float32)).astype(x.dtype)


def alibi_slopes(n):
    return jnp.asarray(np.array([2.0 ** (-8.0 * (i + 1) / n) for i in range(n)], np.float32))


def swiglu(x, w1, w3, w2):
    return (jax.nn.silu(x @ w1) * (x @ w3)) @ w2


def map_query_blocks(fn, qb, *qs):
    T = qs[0].shape[1]
    nb = T // qb
    xs = tuple(jnp.moveaxis(a.reshape((a.shape[0], nb, qb) + a.shape[2:]), 1, 0) for a in qs)
    outs = lax.map(lambda args: fn(args[0], *args[1]), (jnp.arange(nb), xs))
    return tuple(jnp.moveaxis(o, 0, 1).reshape((o.shape[1], T) + o.shape[3:]) for o in outs)


def even_attend(qa, qb, fq, q_pos, segs, lam, slopes):
    qa1, qa2 = qa[..., :DQK_A], qa[..., DQK_A:]
    fqt = jnp.swapaxes(fq, 1, 2)[..., :, None]
    s1l, s2l, sbl = [], [], []
    for ka, va, kb, vb, fk, k_pos in segs:
        dist = (q_pos[:, None] - k_pos[None, :]).astype(jnp.float32)
        causal = dist >= 0
        alibi = -slopes[:, None, None] * dist
        s1 = jnp.einsum('bqhd,bkhd->bhqk', qa1, ka[..., :DQK_A]).astype(jnp.float32) * SCALE_A + alibi
        s2 = jnp.einsum('bqhd,bkhd->bhqk', qa2, ka[..., DQK_A:]).astype(jnp.float32) * SCALE_A + alibi
        sb = (jnp.einsum('bqhd,bkhd->bhqk', qb, kb).astype(jnp.float32) * SCALE_B
              + (fqt - jnp.swapaxes(fk, 1, 2)[..., None, :]))
        s1l.append(jnp.where(causal, s1, NEG))
        s2l.append(jnp.where(causal, s2, NEG))
        sbl.append(jnp.where(causal, sb, NEG))
    p1 = jax.nn.softmax(jnp.concatenate(s1l, axis=-1), axis=-1)
    p2 = jax.nn.softmax(jnp.concatenate(s2l, axis=-1), axis=-1)
    pa = p1 - lam * p2
    pb = jax.nn.softmax(jnp.concatenate(sbl, axis=-1), axis=-1)
    oa, ob, start = 0.0, 0.0, 0
    for ka, va, kb, vb, fk, k_pos in segs:
        n = k_pos.shape[0]
        oa = oa + jnp.einsum('bhqk,bkhd->bqhd', pa[..., start:start + n].astype(va.dtype), va)
        ob = ob + jnp.einsum('bhqk,bkhd->bqhd', pb[..., start:start + n].astype(vb.dtype), vb)
        start += n
    return (oa, ob)


def even_mixer(hn, w_in, b_f, lam_qk, subln_w, w_out, layer_idx, past):
    B, T, _ = hn.shape
    proj = hn @ w_in
    ia, ib = H_A * HD, H_B * HD
    qa = proj[..., :ia].reshape(B, T, H_A, HD)
    ka = proj[..., ia:2 * ia].reshape(B, T, H_A, HD)
    va = proj[..., 2 * ia:3 * ia].reshape(B, T, H_A, HD)
    o0 = 3 * ia
    qb = proj[..., o0:o0 + ib].reshape(B, T, H_B, HD)
    kb = proj[..., o0 + ib:o0 + 2 * ib].reshape(B, T, H_B, HD)
    vb = proj[..., o0 + 2 * ib:o0 + 3 * ib].reshape(B, T, H_B, HD)
    logf = jax.nn.log_sigmoid((proj[..., o0 + 3 * ib:] + b_f).astype(jnp.float32))
    lam_init = 0.8 - 0.6 * math.exp(-0.3 * layer_idx)
    lq = lam_qk.astype(jnp.float32)
    lam = jnp.exp(jnp.sum(lq[0] * lq[1])) - jnp.exp(jnp.sum(lq[2] * lq[3])) + lam_init
    slopes = alibi_slopes(H_A)
    if past is None:
        F = jnp.cumsum(logf, axis=1)
        segs = [(ka, va, kb, vb, F, jnp.arange(T))]
        oa, ob = map_query_blocks(
            lambda blk, qa_b, qb_b, f_b: even_attend(qa_b, qb_b, f_b, blk * Q_BLOCK + jnp.arange(Q_BLOCK), segs, lam, slopes),
            Q_BLOCK, qa, qb, F)
    else:
        a_past, b_past, logf_past = past
        P = logf_past.shape[1]
        F = jnp.cumsum(jnp.concatenate([logf_past.astype(jnp.float32), logf], axis=1), axis=1)
        q_pos = P + jnp.arange(T)
        segs = [(a_past[:, :, 0], a_past[:, :, 1], b_past[:, :, 0], b_past[:, :, 1], F[:, :P], jnp.arange(P)),
                (ka, va, kb, vb, F[:, P:], q_pos)]
        oa, ob = even_attend(qa, qb, F[:, P:], q_pos, segs, lam, slopes)
    oa = rmsnorm(oa, subln_w) * (1.0 - lam_init)
    o = jnp.concatenate([oa, ob.astype(oa.dtype)], axis=2).reshape(B, T, D_MODEL) @ w_out
    return o.astype(hn.dtype), (jnp.stack([ka, va], axis=2), jnp.stack([kb, vb], axis=2), logf)


def compress(kx, pe, w1, w2, b2):
    B, T = kx.shape[:2]
    n_chunk = T // S_CMP
    n_cmp = n_chunk - 1
    chunks = kx[:, :n_chunk * S_CMP].reshape(B, n_chunk, S_CMP, G_C, HD)
    lo = jnp.einsum('bnsgd,sdh->bngh', chunks, w1[:S_CMP])
    hi = jnp.einsum('bnsgd,sdh->bngh', chunks, w1[S_CMP:])
    h = lo[:, :n_cmp] + hi[:, 1:] + jnp.einsum('ld,ldh->h', pe, w1)
    return jax.nn.gelu(h) @ w2 + b2


def sel_overlap(n_cmp, n_sel):
    cs = jnp.arange(n_cmp)[:, None] * S_CMP
    ss = jnp.arange(n_sel)[None, :] * L_SEL
    ov = jnp.minimum(cs + L_CMP, ss + L_SEL) - jnp.maximum(cs, ss)
    return jnp.clip(ov, 0).astype(jnp.float32) / L_CMP


def nsa_attend(q, gate, q_pos, ck, cv, c_end, ovl, sk, sv, wk, wv, w_pos, slopes):
    B, Tq = q.shape[:2]
    T = sk.shape[2]
    qg = q.reshape(B, Tq, G_C, R_C, HD)
    sl = slopes.reshape(G_C, R_C)[:, :, None, None]
    dist_c = (q_pos[:, None] - c_end[None, :]).astype(jnp.float32)
    ok_c = dist_c >= 0
    s = jnp.einsum('bqgrd,bngd->bgrqn', qg, ck).astype(jnp.float32) * SCALE_B - sl * dist_c
    p_c = jnp.where(ok_c, jax.nn.softmax(jnp.where(ok_c, s, NEG), axis=-1), 0.0)
    o_c = jnp.einsum('bgrqn,bngd->bqgrd', p_c.astype(cv.dtype), cv)
    n_sel = ovl.shape[1]
    n_top = min(N_TOP, n_sel)
    imp = jnp.einsum('bgqn,nj->bgqj', jnp.sum(p_c, axis=2), ovl)
    blk = jnp.arange(n_sel)[None, :]
    cur = (q_pos // L_SEL)[:, None]
    valid = blk * L_SEL <= q_pos[:, None]
    forced = ((blk == 0) | (blk == cur) | (blk == cur - 1)).astype(jnp.float32)
    score = jnp.where(valid, imp + FORCE * forced, -1.0)
    _, idx = lax.top_k(score, n_top)
    tok = (idx[..., None] * L_SEL + jnp.arange(L_SEL)).reshape(B, G_C, Tq * n_top * L_SEL)
    tok_c = jnp.minimum(tok, T - 1)
    bi = jnp.arange(B)[:, None, None]
    gi = jnp.arange(G_C)[None, :, None]
    kg = sk[bi, gi, tok_c].reshape(B, G_C, Tq, n_top * L_SEL, HD)
    vg = sv[bi, gi, tok_c].reshape(B, G_C, Tq, n_top * L_SEL, HD)
    dist_s = (q_pos[None, None, :, None] - tok.reshape(B, G_C, Tq, n_top * L_SEL)).astype(jnp.float32)[:, :, None]
    s = jnp.einsum('bqgrd,bgqkd->bgrqk', qg, kg).astype(jnp.float32) * SCALE_B - sl * dist_s
    p_s = jax.nn.softmax(jnp.where(dist_s >= 0, s, NEG), axis=-1)
    o_s = jnp.einsum('bgrqk,bgqkd->bqgrd', p_s.astype(vg.dtype), vg)
    dist_w = q_pos[:, None] - w_pos[None, :]
    ok_w = (dist_w >= 0) & (dist_w < WINDOW) & (w_pos[None, :] >= 0)
    s = jnp.einsum('bqgrd,bkgd->bgrqk', qg, wk).astype(jnp.float32) * SCALE_B - sl * dist_w.astype(jnp.float32)
    p_w = jax.nn.softmax(jnp.where(ok_w, s, NEG), axis=-1)
    o_w = jnp.einsum('bgrqk,bkgd->bqgrd', p_w.astype(wv.dtype), wv)
    g = gate.reshape(B, Tq, G_C, R_C, 3)
    o = g[..., 0:1] * o_c + g[..., 1:2] * o_s + g[..., 2:3] * o_w
    return (o.reshape(B, Tq, H_C, HD).astype(q.dtype),)


def nsa_mixer(hn, w_in, b_gate, pe, w1, w2, b2, w_out, past_kv, past_win):
    B, T, _ = hn.shape
    proj = hn @ w_in
    nq, nkv, nw = H_C * HD, 4 * G_C * HD, 2 * G_C * HD
    q = proj[..., :nq].reshape(B, T, H_C, HD)
    kv4 = proj[..., nq:nq + nkv].reshape(B, T, 4, G_C, HD)
    wkv = proj[..., nq + nkv:nq + nkv + nw].reshape(B, T, 2, G_C, HD)
    gate = jax.nn.sigmoid((proj[..., nq + nkv + nw:] + b_gate).astype(jnp.float32)).reshape(B, T, H_C, 3)
    slopes = alibi_slopes(H_C)
    if past_kv is None:
        P = 0
        full = kv4
    else:
        P = past_kv.shape[1]
        full = jnp.concatenate([past_kv, kv4], axis=1)
    Tf = P + T
    ck = compress(full[:, :, 0], pe[0], w1[0], w2[0], b2[0])
    cv = compress(full[:, :, 1], pe[1], w1[1], w2[1], b2[1])
    n_cmp = ck.shape[1]
    c_end = jnp.arange(n_cmp) * S_CMP + L_CMP - 1
    ovl = sel_overlap(n_cmp, -(-Tf // L_SEL))
    sk = jnp.moveaxis(full[:, :, 2], 1, 2)
    sv = jnp.moveaxis(full[:, :, 3], 1, 2)
    if past_win is None:
        wpad = jnp.pad(wkv, ((0, 0), (WINDOW, 0), (0, 0), (0, 0), (0, 0)))

        def blockfn(blk, q_b, g_b):
            start = blk * NSA_QB
            band = lax.dynamic_slice_in_dim(wpad, start, NSA_QB + WINDOW, axis=1)
            w_pos = start - WINDOW + jnp.arange(NSA_QB + WINDOW)
            return nsa_attend(q_b, g_b, start + jnp.arange(NSA_QB), ck, cv, c_end, ovl, sk, sv,
                              band[:, :, 0], band[:, :, 1], w_pos, slopes)

        (o,) = map_query_blocks(blockfn, NSA_QB, q, gate)
        new_win = wkv[:, -min(WINDOW, T):]
    else:
        wb = past_win.shape[1]
        wall = jnp.concatenate([past_win, wkv], axis=1)
        w_pos = P - wb + jnp.arange(wb + T)
        (o,) = nsa_attend(q, gate, P + jnp.arange(T), ck, cv, c_end, ovl, sk, sv,
                          wall[:, :, 0], wall[:, :, 1], w_pos, slopes)
        new_win = wall[:, -wb:]
    o = o.reshape(B, T, D_MODEL) @ w_out
    return o.astype(hn.dtype), (kv4, new_win)


def setup_inputs(seed: int = 0) -> dict:
    key = jax.random.key(seed)
    ks = jax.random.split(key, 32)
    f32 = jnp.float32
    n_pages = PAST_LEN // PAGE_SIZE
    n_pool = (DEC_BATCH * n_pages * 5) // 4
    win_buf = min(WINDOW, PAST_LEN)

    def nrm(k, shape, scale=1.0):
        return jax.random.normal(k, shape, f32) * scale

    page_table = jax.random.permutation(ks[0], n_pool)[:DEC_BATCH * n_pages].reshape(DEC_BATCH, n_pages).astype(jnp.int32)
    return {
        'x_prompt': nrm(ks[1], (BATCH, SEQ, D_MODEL)),
        'x_sample': nrm(ks[2], (DEC_BATCH, DEC_SEQ, D_MODEL)),
        'cache_a_kv': nrm(ks[3], (N_EVEN, n_pool, PAGE_SIZE, 2, H_A, HD)),
        'cache_b_kv': nrm(ks[4], (N_EVEN, n_pool, PAGE_SIZE, 2, H_B, HD)),
        'cache_b_logf': jax.nn.log_sigmoid(2.0 + nrm(ks[5], (N_EVEN, n_pool, PAGE_SIZE, H_B))),
        'cache_nsa_kv': nrm(ks[6], (N_ODD, n_pool, PAGE_SIZE, 4, G_C, HD)),
        'state_win_kv': nrm(ks[7], (N_ODD, DEC_BATCH, win_buf, 2, G_C, HD)),
        'page_table': page_table,
        'norm_mix': 1.0 + nrm(ks[8], (DEPTH, D_MODEL), 0.05),
        'norm_ffn': 1.0 + nrm(ks[9], (DEPTH, D_MODEL), 0.05),
        'norm_final': 1.0 + nrm(ks[10], (D_MODEL,), 0.05),
        'even_w_in': nrm(ks[11], (N_EVEN, D_MODEL, W_EVEN), D_MODEL ** -0.5),
        'even_b_f': 2.0 + nrm(ks[12], (N_EVEN, H_B), 0.1),
        'diff_lambda': nrm(ks[13], (N_EVEN, 4, DQK_A), 0.1),
        'diff_subln': 1.0 + nrm(ks[14], (N_EVEN, HD), 0.05),
        'even_w_out': nrm(ks[15], (N_EVEN, D_MODEL, D_MODEL), D_MODEL ** -0.5),
        'odd_w_in': nrm(ks[16], (N_ODD, D_MODEL, W_ODD), D_MODEL ** -0.5),
        'odd_b_gate': nrm(ks[17], (N_ODD, 3 * H_C), 0.1),
        'cmp_pe': nrm(ks[18], (N_ODD, 2, L_CMP, HD), 0.1),
        'cmp_w1': nrm(ks[19], (N_ODD, 2, L_CMP, HD, CMP_HID), (L_CMP * HD) ** -0.5),
        'cmp_w2': nrm(ks[20], (N_ODD, 2, CMP_HID, HD), CMP_HID ** -0.5),
        'cmp_b2': nrm(ks[21], (N_ODD, 2, HD), 0.01),
        'odd_w_out': nrm(ks[22], (N_ODD, D_MODEL, D_MODEL), D_MODEL ** -0.5),
        'ffn_w1': nrm(ks[23], (DEPTH, D_MODEL, D_FF), D_MODEL ** -0.5),
        'ffn_w3': nrm(ks[24], (DEPTH, D_MODEL, D_FF), D_MODEL ** -0.5),
        'ffn_w2': nrm(ks[25], (DEPTH, D_FF, D_MODEL), D_FF ** -0.5),
    }


def reference(x_prompt, x_sample, cache_a_kv, cache_b_kv, cache_b_logf, cache_nsa_kv, state_win_kv, page_table,
              norm_mix, norm_ffn, norm_final, even_w_in, even_b_f, diff_lambda, diff_subln, even_w_out,
              odd_w_in, odd_b_gate, cmp_pe, cmp_w1, cmp_w2, cmp_b2, odd_w_out, ffn_w1, ffn_w3, ffn_w2):
    def gather(cache, li):
        g = cache[li, page_table]
        return g.reshape((g.shape[0], g.shape[1] * g.shape[2]) + g.shape[3:])

    hp, hs = x_prompt, x_sample
    a_p, a_s, b_p, b_s, lf_p, lf_s = [], [], [], [], [], []
    n_p, n_s, w_p, w_s = [], [], [], []
    for l in range(DEPTH):
        if l % 2 == 0:
            e = l // 2
            args = (even_w_in[e], even_b_f[e], diff_lambda[e], diff_subln[e], even_w_out[e], l)
            op, (akp, bkp, lfp) = even_mixer(rmsnorm(hp, norm_mix[l]), *args, None)
            os_, (aks, bks, lfs) = even_mixer(rmsnorm(hs, norm_mix[l]), *args,
                                              (gather(cache_a_kv, e), gather(cache_b_kv, e), gather(cache_b_logf, e)))
            a_p.append(akp); b_p.append(bkp); lf_p.append(lfp)
            a_s.append(aks); b_s.append(bks); lf_s.append(lfs)
        else:
            o = l // 2
            args = (odd_w_in[o], odd_b_gate[o], cmp_pe[o], cmp_w1[o], cmp_w2[o], cmp_b2[o], odd_w_out[o])
            op, (kvp, winp) = nsa_mixer(rmsnorm(hp, norm_mix[l]), *args, None, None)
            os_, (kvs, wins) = nsa_mixer(rmsnorm(hs, norm_mix[l]), *args, gather(cache_nsa_kv, o), state_win_kv[o])
            n_p.append(kvp); w_p.append(winp)
            n_s.append(kvs); w_s.append(wins)
        hp = hp + op
        hs = hs + os_
        hp = hp + swiglu(rmsnorm(hp, norm_ffn[l]), ffn_w1[l], ffn_w3[l], ffn_w2[l])
        hs = hs + swiglu(rmsnorm(hs, norm_ffn[l]), ffn_w1[l], ffn_w3[l], ffn_w2[l])
    y_prompt = rmsnorm(hp, norm_final)
    y_sample = rmsnorm(hs, norm_final)
    return (y_prompt, y_sample,
            jnp.stack(a_p), jnp.stack(b_p), jnp.stack(lf_p), jnp.stack(n_p), jnp.stack(w_p),
            jnp.stack(a_s), jnp.stack(b_s), jnp.stack(lf_s), jnp.stack(n_s), jnp.stack(w_s))
```

```python
import functools
import math

import numpy as np
import jax
import jax.numpy as jnp
from jax import lax
from jax.experimental import pallas as pl
from jax.experimental.pallas import tpu as pltpu

D_MODEL = 2048
DEPTH = 4
PAGE_SIZE = 128
HD = 128
H_A = D_MODEL // (2 * HD)
H_B = D_MODEL // (2 * HD)
DQK_A = HD // 2
H_C = D_MODEL // HD
G_C = 4
R_C = H_C // G_C
L_CMP = 32
S_CMP = 16
L_SEL = 64
N_TOP = 16
WINDOW = 512
D_FF = -(-(8 * D_MODEL) // (3 * 256)) * 256
EPS = 1e-6
NEG = -1e30
FORCE = 1e4
SCALE_A = DQK_A ** -0.5
SCALE_B = HD ** -0.5

F32 = jnp.float32
BF16 = jnp.bfloat16

VMEM_LIMIT_V7X = 56 * 1024 * 1024
ROW_TILE = 768
ROW_ALIGN = 768


def _cparams(sem):
    return pltpu.CompilerParams(dimension_semantics=sem, vmem_limit_bytes=VMEM_LIMIT_V7X)


def _alibi_slopes(n):
    return np.array([2.0 ** (-8.0 * (i + 1) / n) for i in range(n)], np.float32)


def _rmsnorm_kernel(x_ref, g_ref, o_ref):
    x = x_ref[...]
    y = x * lax.rsqrt(jnp.mean(x * x, axis=-1, keepdims=True) + EPS)
    o_ref[...] = (y * g_ref[...]).astype(o_ref.dtype)


def rmsnorm_rows(x, g, out_dtype, tm):
    m, d = x.shape
    return pl.pallas_call(
        _rmsnorm_kernel,
        grid=(m // tm,),
        in_specs=[pl.BlockSpec((tm, d), lambda i: (i, 0)),
                  pl.BlockSpec((1, d), lambda i: (0, 0))],
        out_specs=pl.BlockSpec((tm, d), lambda i: (i, 0)),
        out_shape=jax.ShapeDtypeStruct((m, d), out_dtype),
        compiler_params=_cparams(("parallel",)),
        name="rmsnorm",
    )(x, g.reshape(1, d).astype(F32))


def _matmul_kernel(*refs, n_split, has_scale, has_res, n_out):
    x_ref, w_ref = refs[0], refs[1]
    pos = 2
    scale_ref = res_ref = None
    if has_scale:
        scale_ref = refs[pos]
        pos += 1
    if has_res:
        res_ref = refs[pos]
        pos += 1
    out_refs = refs[pos:pos + n_out]
    acc = None
    for s in range(n_split):
        part = jnp.dot(x_ref[s], w_ref[s], preferred_element_type=F32)
        acc = part if acc is None else acc + part
    if has_scale:
        acc = acc * scale_ref[...]
    if has_res:
        acc = acc + res_ref[...]
    for o_ref in out_refs:
        o_ref[...] = acc.astype(o_ref.dtype)


def matmul_rows(x, w, col0, ncols, out_dtypes, *, scale=None, res=None, tm, tn):
    n_split, m, kc = x.shape
    assert w.shape[0] == n_split and w.shape[1] == kc
    assert col0 % tn == 0 and ncols % tn == 0 and m % tm == 0
    cb0 = col0 // tn
    in_specs = [pl.BlockSpec((n_split, tm, kc), lambda j, i: (0, i, 0)),
                pl.BlockSpec((n_split, kc, tn), lambda j, i: (0, 0, cb0 + j))]
    args = [x, w]
    if scale is not None:
        in_specs.append(pl.BlockSpec((1, tn), lambda j, i: (0, j)))
        args.append(scale.reshape(1, ncols).astype(F32))
    if res is not None:
        in_specs.append(pl.BlockSpec((tm, tn), lambda j, i: (i, j)))
        args.append(res)
    out_specs = [pl.BlockSpec((tm, tn), lambda j, i: (i, j)) for _ in out_dtypes]
    out_shape = [jax.ShapeDtypeStruct((m, ncols), dt) for dt in out_dtypes]
    outs = pl.pallas_call(
        functools.partial(_matmul_kernel, n_split=n_split, has_scale=scale is not None,
                          has_res=res is not None, n_out=len(out_dtypes)),
        grid=(ncols // tn, m // tm),
        in_specs=in_specs,
        out_specs=out_specs,
        out_shape=out_shape,
        compiler_params=_cparams(("parallel", "parallel")),
        name="matmul_rows",
    )(*args)
    return outs


def _ffn_kernel(x_ref, w1_ref, w3_ref, w2_ref, res_ref, o_ref):
    f = pl.program_id(1)

    @pl.when(f == 0)
    def _():
        o_ref[...] = res_ref[...]

    x = x_ref[...]
    a = jnp.dot(x, w1_ref[...], preferred_element_type=F32)
    b = jnp.dot(x, w3_ref[...], preferred_element_type=F32)
    h = (a * jax.nn.sigmoid(a) * b).astype(BF16)
    o_ref[...] += jnp.dot(h, w2_ref[...], preferred_element_type=F32)


def ffn_rows(xn, w1, w3, w2, res, *, tm, tf):
    m, d = xn.shape
    dff = w1.shape[1]
    assert m % tm == 0 and dff % tf == 0
    return pl.pallas_call(
        _ffn_kernel,
        grid=(m // tm, dff // tf),
        in_specs=[pl.BlockSpec((tm, d), lambda i, f: (i, 0)),
                  pl.BlockSpec((d, tf), lambda i, f: (0, f)),
                  pl.BlockSpec((d, tf), lambda i, f: (0, f)),
                  pl.BlockSpec((tf, d), lambda i, f: (f, 0)),
                  pl.BlockSpec((tm, d), lambda i, f: (i, 0))],
        out_specs=pl.BlockSpec((tm, d), lambda i, f: (i, 0)),
        out_shape=jax.ShapeDtypeStruct((m, d), F32),
        compiler_params=_cparams(("parallel", "arbitrary")),
        name="swiglu_ffn",
    )(xn, w1, w3, w2, res)


def _split3_bf16(x):
    hi = x.astype(BF16)
    r1 = x - hi.astype(F32)
    mid = r1.astype(BF16)
    lo = (r1 - mid.astype(F32)).astype(BF16)
    return hi, mid, lo


def _suffix_sum_kernel(x_ref, o_ref, *, n_chunk):
    rows = x_ref.shape[1]
    tri = (lax.broadcasted_iota(jnp.int32, (128, 128), 0)
           > lax.broadcasted_iota(jnp.int32, (128, 128), 1)).astype(BF16)

    def body(k, carry):
        c = n_chunk - 1 - k
        x = x_ref[c]
        hi, mid, lo = _split3_bf16(x)
        s = (jnp.dot(hi, tri, preferred_element_type=F32)
             + jnp.dot(mid, tri, preferred_element_type=F32)
             + jnp.dot(lo, tri, preferred_element_type=F32)) + carry
        o_ref[c] = s
        return s[:, 0:1] + x[:, 0:1]

    lax.fori_loop(0, n_chunk, body, jnp.zeros((rows, 1), F32))


def suffix_sum_rows(x):
    r, t = x.shape
    n_chunk = t // 128
    x3 = jnp.transpose(x.reshape(r, n_chunk, 128), (1, 0, 2))
    out = pl.pallas_call(
        functools.partial(_suffix_sum_kernel, n_chunk=n_chunk),
        out_shape=jax.ShapeDtypeStruct((n_chunk, r, 128), F32),
        compiler_params=pltpu.CompilerParams(vmem_limit_bytes=VMEM_LIMIT_V7X),
        name="suffix_sum",
    )(x3)
    return jnp.transpose(out, (1, 0, 2)).reshape(r, t)


_NT = (((1,), (1,)), ((), ()))


def _softmax_step(q, k, v, bias, mask, state):
    m, l, acc = state
    s = lax.dot_general(q, k, _NT, preferred_element_type=F32) + bias
    if mask is not None:
        s = jnp.where(mask, s, NEG)
    m_new = jnp.maximum(m, jnp.max(s, axis=1, keepdims=True))
    alpha = jnp.exp(m - m_new)
    p = jnp.exp(s - m_new)
    l = alpha * l + jnp.sum(p, axis=1, keepdims=True)
    acc = alpha * acc + jnp.dot(p.astype(BF16), v, preferred_element_type=F32)
    return m_new, l, acc


def _even_attn_kernel(slope_ref, lam_ref, qa_ref, ka_ref, va_ref, qb_ref, kb_ref, vb_ref,
                      sfx_ref, sub_ref, o_ref, *, out_scale):
    h = pl.program_id(1)
    i = pl.program_id(2)
    tq = qa_ref.shape[0]
    tk = tq
    slope = slope_ref[h]
    lam = lam_ref[0]
    qa = qa_ref[...]
    lane = lax.broadcasted_iota(jnp.int32, (1, HD), 1)
    q1 = jnp.where(lane < DQK_A, qa, jnp.zeros_like(qa))
    q2 = jnp.where(lane >= DQK_A, qa, jnp.zeros_like(qa))
    qb = qb_ref[...]
    col = lax.broadcasted_iota(jnp.int32, (1, tk), 1).astype(F32)
    sfx_q0 = sfx_ref[0, 0, pl.ds(i, 1), :][:, 0:1]

    def tile(j, states, mask):
        r0 = pl.multiple_of(j * tk, tk)
        k_a = ka_ref[pl.ds(r0, tk), :]
        v_a = va_ref[pl.ds(r0, tk), :]
        k_b = kb_ref[pl.ds(r0, tk), :]
        v_b = vb_ref[pl.ds(r0, tk), :]
        bias_a = slope * (((j - i) * tk).astype(F32) + col)
        bias_b = sfx_ref[0, 0, pl.ds(j, 1), :] - sfx_q0
        s1 = _softmax_step(q1, k_a, v_a, bias_a, mask, states[0])
        s2 = _softmax_step(q2, k_a, v_a, bias_a, mask, states[1])
        sb = _softmax_step(qb, k_b, v_b, bias_b, mask, states[2])
        return (s1, s2, sb)

    init = tuple((jnp.full((tq, 1), NEG, F32), jnp.zeros((tq, 1), F32), jnp.zeros((tq, HD), F32))
                 for _ in range(3))
    states = lax.fori_loop(0, i, lambda j, st: tile(j, st, None), init)
    diag = (lax.broadcasted_iota(jnp.int32, (tq, tk), 0) >= lax.broadcasted_iota(jnp.int32, (tq, tk), 1))
    (m1, l1, a1), (m2, l2, a2), (mb, lb, ab) = tile(i, states, diag)
    oa = a1 / l1 - lam * (a2 / l2)
    y = oa * lax.rsqrt(jnp.mean(oa * oa, axis=-1, keepdims=True) + EPS)
    y = (y * sub_ref[...]) * out_scale
    o_ref[0] = y.astype(o_ref.dtype)
    o_ref[1] = (ab / lb).astype(o_ref.dtype)


def even_attention_prompt(pb, sfx, slopes, lam, subln, lam_init, *, batch, seq, tq):
    nq = seq // tq
    hb = 1024 // HD
    sfx4 = sfx.reshape(batch, H_B, nq, tq)
    smem = pl.BlockSpec(memory_space=pltpu.SMEM)
    qspec = lambda c0: pl.BlockSpec((tq, HD), lambda b, h, i: (b * nq + i, c0 + h))
    kspec = lambda c0: pl.BlockSpec((seq, HD), lambda b, h, i: (b, c0 + h))
    return pl.pallas_call(
        functools.partial(_even_attn_kernel, out_scale=1.0 - lam_init),
        grid=(batch, H_A, nq),
        in_specs=[smem, smem, qspec(0), kspec(hb), kspec(2 * hb), qspec(3 * hb), kspec(4 * hb), kspec(5 * hb),
                  pl.BlockSpec((1, 1, nq, tq), lambda b, h, i: (b, h, 0, 0)),
                  pl.BlockSpec((1, HD), lambda b, h, i: (0, 0))],
        out_specs=pl.BlockSpec((2, tq, HD), lambda b, h, i: (0, b * nq + i, h)),
        out_shape=jax.ShapeDtypeStruct((2, batch * seq, H_A * HD), BF16),
        compiler_params=_cparams(("parallel", "parallel", "arbitrary")),
        name="even_attention_prompt",
    )(slopes, lam.reshape(1).astype(F32), pb, pb, pb, pb, pb, pb, sfx4, subln.reshape(1, HD).astype(F32))


def _compress_kernel(x_ref, w1_ref, w1f_ref, pe_ref, w2_ref, b2_ref, o_ref, *, n_chunk):
    acc = jnp.zeros((n_chunk, 2 * HD), F32)
    for s in range(S_CMP):
        xs = x_ref[pl.ds(s, n_chunk, stride=S_CMP), :].astype(BF16)
        acc = acc + jnp.dot(xs, w1_ref[0, s], preferred_element_type=F32)
    pe8 = jnp.broadcast_to(pe_ref[0], (8, L_CMP * HD)).astype(BF16)
    pe_term = jnp.dot(pe8, w1f_ref[0], preferred_element_type=F32)[0:1, :]
    lo = acc[:, :HD]
    hi_next = pltpu.roll(acc[:, HD:], n_chunk - 1, 0)
    hid = jax.nn.gelu(lo + hi_next + pe_term)
    out = jnp.dot(hid.astype(BF16), w2_ref[0], preferred_element_type=F32) + b2_ref[0]
    o_ref[0, 0, 0] = out.astype(o_ref.dtype)


def compress_blocks(x2d, col_block0, w1cat, w1flat, pe_flat, w2, b2, *, batch, seq):
    n_chunk = seq // S_CMP
    return pl.pallas_call(
        functools.partial(_compress_kernel, n_chunk=n_chunk),
        grid=(batch, 2, G_C),
        in_specs=[pl.BlockSpec((seq, HD), lambda b, c, g: (b, col_block0 + c * G_C + g)),
                  pl.BlockSpec((1, S_CMP, HD, 2 * HD), lambda b, c, g: (c, 0, 0, 0)),
                  pl.BlockSpec((1, L_CMP * HD, HD), lambda b, c, g: (c, 0, 0)),
                  pl.BlockSpec((1, 1, L_CMP * HD), lambda b, c, g: (c, 0, 0)),
                  pl.BlockSpec((1, HD, HD), lambda b, c, g: (c, 0, 0)),
                  pl.BlockSpec((1, 1, HD), lambda b, c, g: (c, 0, 0))],
        out_specs=pl.BlockSpec((1, 1, 1, n_chunk, HD), lambda b, c, g: (b, c, g, 0, 0)),
        out_shape=jax.ShapeDtypeStruct((batch, 2, G_C, n_chunk, HD), BF16),
        compiler_params=_cparams(("parallel", "parallel", "parallel")),
        name="nsa_compress",
    )(x2d, w1cat, w1flat, pe_flat, w2, b2)


NSA_TQ = 64
NSA_LANES = R_C * NSA_TQ
SEL_TK = 256
WIN_TK = 128
WIN_CHUNKS = (WINDOW + NSA_TQ + (WIN_TK - NSA_TQ) + WIN_TK - 1) // WIN_TK


def _split2_bf16(x):
    hi = x.astype(BF16)
    lo = (x - hi.astype(F32)).astype(BF16)
    return hi, lo


def _online_step_t(s, v_t, state):
    m, l, acc = state
    m_new = jnp.maximum(m, jnp.max(s, axis=0, keepdims=True))
    alpha = jnp.exp(m - m_new)
    p = jnp.exp(s - m_new)
    l = alpha * l + jnp.sum(p, axis=0, keepdims=True)
    acc = alpha * acc + jnp.dot(v_t, p.astype(BF16), preferred_element_type=F32)
    return m_new, l, acc


def _select_blocks_t(score, n_top):
    n_blk = score.shape[0]
    blk = lax.broadcasted_iota(jnp.int32, score.shape, 0)
    cnt = jnp.zeros(score.shape, F32)
    for ii in range(n_blk):
        row = score[ii:ii + 1, :]
        beats = (row > score) | ((row == score) & (blk > ii))
        cnt = cnt + beats.astype(F32)
    return (cnt < float(n_top)).astype(F32)


def _nsa_prompt_kernel(q_ref, ck_ref, cvt_ref, ovlt_ref, sk_ref, svt_ref, wk_ref, wvt_ref,
                       gate_ref, slope_ref, o_ref, sel_ref, *, seq, n_cmp, n_top):
    i = pl.program_id(2)
    qstart = i * NSA_TQ
    lanes = NSA_LANES
    qg = q_ref[...]
    q4 = jnp.concatenate([qg[:, r * HD:(r + 1) * HD] for r in range(R_C)], axis=0)
    slope = slope_ref[0]
    lane_q = lax.broadcasted_iota(jnp.int32, (1, lanes), 1) & (NSA_TQ - 1)
    qpos = qstart + lane_q
    qpos_f = qpos.astype(F32)

    n_chunk = ck_ref.shape[3]
    ck = ck_ref[0, 0, 0]
    sc = lax.dot_general(ck, q4, _NT, preferred_element_type=F32)
    n_idx = lax.broadcasted_iota(jnp.int32, (n_chunk, lanes), 0)
    dist_c = qpos_f - (n_idx * S_CMP + (L_CMP - 1)).astype(F32)
    ok_c = (dist_c >= 0.0) & (n_idx < n_cmp)
    s = jnp.where(ok_c, sc - slope * dist_c, NEG)
    m_c = jnp.max(s, axis=0, keepdims=True)
    p = jnp.where(ok_c, jnp.exp(s - m_c), 0.0)
    l_c = jnp.sum(p, axis=0, keepdims=True)
    p_c = p / jnp.maximum(l_c, 1e-30)
    o_c = jnp.dot(cvt_ref[0, 0], p_c.astype(BF16), preferred_element_type=F32)

    ovlt = ovlt_ref[...]
    p_hi, p_lo = _split2_bf16(p_c)
    imp_h = (jnp.dot(ovlt, p_hi, preferred_element_type=F32)
             + jnp.dot(ovlt, p_lo, preferred_element_type=F32))
    same_q = ((lax.broadcasted_iota(jnp.int32, (lanes, lanes), 0) & (NSA_TQ - 1))
              == (lax.broadcasted_iota(jnp.int32, (lanes, lanes), 1) & (NSA_TQ - 1))).astype(BF16)
    i_hi, i_lo = _split2_bf16(imp_h)
    imp = (jnp.dot(i_hi, same_q, preferred_element_type=F32)
           + jnp.dot(i_lo, same_q, preferred_element_type=F32))
    n_sel = ovlt.shape[0]
    blk = lax.broadcasted_iota(jnp.int32, (n_sel, lanes), 0)
    cur = lax.shift_right_arithmetic(qpos, int(math.log2(L_SEL)))
    valid = blk * L_SEL <= qpos
    forced = ((blk == 0) | (blk == cur) | (blk == cur - 1)).astype(F32)
    score = jnp.where(valid, imp + FORCE * forced, -1.0)
    sel_ref[...] = _select_blocks_t(score, n_top)

    tok_sub = lax.broadcasted_iota(jnp.int32, (SEL_TK, lanes), 0)
    blocks_per_tile = SEL_TK // L_SEL

    def sel_tile(jt, state, causal):
        r0 = pl.multiple_of(jt * SEL_TK, SEL_TK)
        k = sk_ref[pl.ds(r0, SEL_TK), :]
        sc_s = lax.dot_general(k, q4, _NT, preferred_element_type=F32)
        tok = jt * SEL_TK + tok_sub
        sc_s = sc_s + slope * (tok - qstart).astype(F32)
        rows = [jnp.broadcast_to(sel_ref[pl.ds(jt * blocks_per_tile + bi, 1), :], (L_SEL, lanes))
                for bi in range(blocks_per_tile)]
        keep = jnp.concatenate(rows, axis=0) > 0.5
        if causal:
            keep = keep & (tok <= qpos)
        sc_s = jnp.where(keep, sc_s, NEG)
        return _online_step_t(sc_s, svt_ref[0, 0, jt], state)

    init = (jnp.full((1, lanes), NEG, F32), jnp.zeros((1, lanes), F32), jnp.zeros((HD, lanes), F32))
    j_last = qstart // SEL_TK
    st = lax.fori_loop(0, j_last, lambda jt, s_: sel_tile(jt, s_, False), init)
    _, l_s, a_s = sel_tile(j_last, st, True)
    o_s = a_s / l_s

    n_wchunks_total = seq // WIN_TK
    n_w = min(WIN_CHUNKS, n_wchunks_total)
    c0 = jnp.clip(jnp.maximum(qstart - WINDOW, 0) // WIN_TK, 0, n_wchunks_total - n_w)
    tok_sub_w = lax.broadcasted_iota(jnp.int32, (WIN_TK, lanes), 0)
    st = init
    for cw in range(n_w):
        c = c0 + cw
        r0 = pl.multiple_of(c * WIN_TK, WIN_TK)
        k = wk_ref[pl.ds(r0, WIN_TK), :]
        sc_w = lax.dot_general(k, q4, _NT, preferred_element_type=F32)
        tok = c * WIN_TK + tok_sub_w
        dist = qpos - tok
        sc_w = sc_w + slope * (tok - qstart).astype(F32)
        sc_w = jnp.where((dist >= 0) & (dist < WINDOW), sc_w, NEG)
        st = _online_step_t(sc_w, wvt_ref[0, 0, c], st)
    _, l_w, a_w = st
    o_w = a_w / l_w

    gates = gate_ref[0, 0, 0]
    o_t = gates[0:1, :] * o_c + gates[1:2, :] * o_s + gates[2:3, :] * o_w
    o = o_t.T.astype(o_ref.dtype)
    for r in range(R_C):
        o_ref[:, r * HD:(r + 1) * HD] = o[r * NSA_TQ:(r + 1) * NSA_TQ, :]


def nsa_attention_prompt(qb, pb, col_sk, col_wk, ckv, cvt, ovlt, svt, wvt, gates_t, slope_t, *, batch, seq):
    nq = seq // NSA_TQ
    n_chunk = seq // S_CMP
    n_sel = ovlt.shape[0]
    return pl.pallas_call(
        functools.partial(_nsa_prompt_kernel, seq=seq, n_cmp=n_chunk - 1, n_top=min(N_TOP, n_sel)),
        grid=(batch, G_C, nq),
        in_specs=[pl.BlockSpec((NSA_TQ, R_C * HD), lambda b, g, i: (b * nq + i, g)),
                  pl.BlockSpec((1, 1, 1, n_chunk, HD), lambda b, g, i: (b, 0, g, 0, 0)),
                  pl.BlockSpec((1, 1, HD, n_chunk), lambda b, g, i: (b, g, 0, 0)),
                  pl.BlockSpec((n_sel, n_chunk), lambda b, g, i: (0, 0)),
                  pl.BlockSpec((seq, HD), lambda b, g, i: (b, col_sk + g)),
                  pl.BlockSpec((1, 1, seq // SEL_TK, HD, SEL_TK), lambda b, g, i: (b, g, 0, 0, 0)),
                  pl.BlockSpec((seq, HD), lambda b, g, i: (b, col_wk + g)),
                  pl.BlockSpec((1, 1, seq // WIN_TK, HD, WIN_TK), lambda b, g, i: (b, g, 0, 0, 0)),
                  pl.BlockSpec((1, 1, 1, 3, NSA_LANES), lambda b, g, i: (b, g, i, 0, 0)),
                  pl.BlockSpec((1, 1, NSA_LANES), lambda b, g, i: (g, 0, 0))],
        out_specs=pl.BlockSpec((NSA_TQ, R_C * HD), lambda b, g, i: (b * nq + i, g)),
        out_shape=jax.ShapeDtypeStruct((batch * seq, H_C * HD), BF16),
        scratch_shapes=[pltpu.VMEM((n_sel, NSA_LANES), F32)],
        compiler_params=_cparams(("parallel", "parallel", "arbitrary")),
        name="nsa_attention_prompt",
    )(qb, ckv, cvt, ovlt, pb, svt, pb, wvt, gates_t, slope_t)


def _rmsnorm_jnp(x, g):
    y = x * lax.rsqrt(jnp.mean(x * x, axis=-1, keepdims=True) + EPS)
    return y * g


def _even_decode_attend(qa, qb, ka, va, kb, vb, logf, a_past, b_past, lf_past, lam, slopes):
    bs, ts = qa.shape[:2]
    p_len = lf_past.shape[1]
    f_all = jnp.cumsum(jnp.concatenate([lf_past, logf], axis=1), axis=1)
    q_pos = p_len + jnp.arange(ts)
    fq = jnp.swapaxes(f_all[:, p_len:], 1, 2)[..., :, None]
    segs = [(a_past[:, :, 0], a_past[:, :, 1], b_past[:, :, 0], b_past[:, :, 1], f_all[:, :p_len], jnp.arange(p_len)),
            (ka, va, kb, vb, f_all[:, p_len:], q_pos)]
    s1l, s2l, sbl = [], [], []
    for k_a, v_a, k_b, v_b, fk, k_pos in segs:
        dist = (q_pos[:, None] - k_pos[None, :]).astype(F32)
        causal = dist >= 0
        alibi = -slopes[:, None, None] * dist
        s1 = jnp.einsum('bqhd,bkhd->bhqk', qa[..., :DQK_A], k_a[..., :DQK_A]) + alibi
        s2 = jnp.einsum('bqhd,bkhd->bhqk', qa[..., DQK_A:], k_a[..., DQK_A:]) + alibi
        sb = jnp.einsum('bqhd,bkhd->bhqk', qb, k_b) + (fq - jnp.swapaxes(fk, 1, 2)[..., None, :])
        s1l.append(jnp.where(causal, s1, NEG))
        s2l.append(jnp.where(causal, s2, NEG))
        sbl.append(jnp.where(causal, sb, NEG))
    p1 = jax.nn.softmax(jnp.concatenate(s1l, axis=-1), axis=-1)
    p2 = jax.nn.softmax(jnp.concatenate(s2l, axis=-1), axis=-1)
    pa = p1 - lam * p2
    pb_ = jax.nn.softmax(jnp.concatenate(sbl, axis=-1), axis=-1)
    oa, ob, start = 0.0, 0.0, 0
    for k_a, v_a, k_b, v_b, fk, k_pos in segs:
        n = k_pos.shape[0]
        oa = oa + jnp.einsum('bhqk,bkhd->bqhd', pa[..., start:start + n], v_a)
        ob = ob + jnp.einsum('bhqk,bkhd->bqhd', pb_[..., start:start + n], v_b)
        start += n
    return oa, ob


def _compress_jnp(kx, pe, w1, w2, b2):
    b, t = kx.shape[:2]
    n_chunk = t // S_CMP
    n_cmp = n_chunk - 1
    chunks = kx[:, :n_chunk * S_CMP].reshape(b, n_chunk, S_CMP, G_C, HD)
    lo = jnp.einsum('bnsgd,sdh->bngh', chunks, w1[:S_CMP])
    hi = jnp.einsum('bnsgd,sdh->bngh', chunks, w1[S_CMP:])
    hid = lo[:, :n_cmp] + hi[:, 1:] + jnp.einsum('ld,ldh->h', pe, w1)
    return jax.nn.gelu(hid) @ w2 + b2


def _nsa_decode_attend(q, gate, kv4, wkv, past_kv, past_win, pe, w1, w2, b2, slopes):
    bs, ts = q.shape[:2]
    p_len = past_kv.shape[1]
    full = jnp.concatenate([past_kv, kv4], axis=1)
    tf = p_len + ts
    ck = _compress_jnp(full[:, :, 0], pe[0], w1[0], w2[0], b2[0])
    cv = _compress_jnp(full[:, :, 1], pe[1], w1[1], w2[1], b2[1])
    n_cmp = ck.shape[1]
    c_end = jnp.arange(n_cmp) * S_CMP + L_CMP - 1
    n_sel = -(-tf // L_SEL)
    cs = jnp.arange(n_cmp)[:, None] * S_CMP
    ss = jnp.arange(n_sel)[None, :] * L_SEL
    ovl = jnp.clip(jnp.minimum(cs + L_CMP, ss + L_SEL) - jnp.maximum(cs, ss), 0).astype(F32) / L_CMP
    sk = jnp.moveaxis(full[:, :, 2], 1, 2)
    sv = jnp.moveaxis(full[:, :, 3], 1, 2)
    wb = past_win.shape[1]
    wall = jnp.concatenate([past_win, wkv], axis=1)
    w_pos = p_len - wb + jnp.arange(wb + ts)
    q_pos = p_len + jnp.arange(ts)
    wk, wv = wall[:, :, 0], wall[:, :, 1]
    qg = q.reshape(bs, ts, G_C, R_C, HD)
    sl = slopes.reshape(G_C, R_C)[:, :, None, None]
    dist_c = (q_pos[:, None] - c_end[None, :]).astype(F32)
    ok_c = dist_c >= 0
    s = jnp.einsum('bqgrd,bngd->bgrqn', qg, ck) - sl * dist_c
    p_c = jnp.where(ok_c, jax.nn.softmax(jnp.where(ok_c, s, NEG), axis=-1), 0.0)
    o_c = jnp.einsum('bgrqn,bngd->bqgrd', p_c, cv)
    n_top = min(N_TOP, n_sel)
    imp = jnp.einsum('bgqn,nj->bgqj', jnp.sum(p_c, axis=2), ovl)
    blk = jnp.arange(n_sel)[None, :]
    cur = (q_pos // L_SEL)[:, None]
    valid = blk * L_SEL <= q_pos[:, None]
    forced = ((blk == 0) | (blk == cur) | (blk == cur - 1)).astype(F32)
    score = jnp.where(valid, imp + FORCE * forced, -1.0)
    _, idx = lax.top_k(score, n_top)
    tok = (idx[..., None] * L_SEL + jnp.arange(L_SEL)).reshape(bs, G_C, ts * n_top * L_SEL)
    tok_c = jnp.minimum(tok, tf - 1)
    bi = jnp.arange(bs)[:, None, None]
    gi = jnp.arange(G_C)[None, :, None]
    kg = sk[bi, gi, tok_c].reshape(bs, G_C, ts, n_top * L_SEL, HD)
    vg = sv[bi, gi, tok_c].reshape(bs, G_C, ts, n_top * L_SEL, HD)
    dist_s = (q_pos[None, None, :, None] - tok.reshape(bs, G_C, ts, n_top * L_SEL)).astype(F32)[:, :, None]
    s = jnp.einsum('bqgrd,bgqkd->bgrqk', qg, kg) - sl * dist_s
    p_s = jax.nn.softmax(jnp.where(dist_s >= 0, s, NEG), axis=-1)
    o_s = jnp.einsum('bgrqk,bgqkd->bqgrd', p_s, vg)
    dist_w = q_pos[:, None] - w_pos[None, :]
    ok_w = (dist_w >= 0) & (dist_w < WINDOW) & (w_pos[None, :] >= 0)
    s = jnp.einsum('bqgrd,bkgd->bgrqk', qg, wk) - sl * dist_w.astype(F32)
    p_w = jax.nn.softmax(jnp.where(ok_w, s, NEG), axis=-1)
    o_w = jnp.einsum('bgrqk,bkgd->bqgrd', p_w, wv)
    g = gate.reshape(bs, ts, G_C, R_C, 3)
    o = g[..., 0:1] * o_c + g[..., 1:2] * o_s + g[..., 2:3] * o_w
    return o.reshape(bs, ts, H_C, HD), wall[:, -wb:]


def _round_up(x, m):
    return -(-x // m) * m


def _sel_overlap_t(n_chunk, n_sel):
    cs = np.arange(n_chunk)[None, :] * S_CMP
    ss = np.arange(n_sel)[:, None] * L_SEL
    ov = np.clip(np.minimum(cs + L_CMP, ss + L_SEL) - np.maximum(cs, ss), 0, None).astype(np.float32) / L_CMP
    ov[:, n_chunk - 1] = 0.0
    return jnp.asarray(ov, BF16)


def _head_major_t(x, batch, seq, tile):
    x = x.reshape(batch, seq // tile, tile, G_C, HD)
    return jnp.transpose(x, (0, 3, 1, 4, 2))


def kernel(x_prompt, x_sample, cache_a_kv, cache_b_kv, cache_b_logf, cache_nsa_kv, state_win_kv, page_table,
           norm_mix, norm_ffn, norm_final, even_w_in, even_b_f, diff_lambda, diff_subln, even_w_out,
           odd_w_in, odd_b_gate, cmp_pe, cmp_w1, cmp_w2, cmp_b2, odd_w_out, ffn_w1, ffn_w3, ffn_w2):
    batch, seq, d = x_prompt.shape
    bs, ts, _ = x_sample.shape
    mp, ms = batch * seq, bs * ts
    m_pad = _round_up(mp + ms, ROW_ALIGN)
    tm = ROW_TILE
    h = jnp.concatenate([x_prompt.reshape(mp, d), x_sample.reshape(ms, d),
                         jnp.zeros((m_pad - mp - ms, d), F32)], axis=0)

    def gather(cache, li):
        g = cache[li, page_table]
        return g.reshape((g.shape[0], g.shape[1] * g.shape[2]) + g.shape[3:])

    ia = H_A * HD
    outs = {k: [] for k in ("a_p", "b_p", "lf_p", "n_p", "w_p", "a_s", "b_s", "lf_s", "n_s", "w_s")}
    for l in range(DEPTH):
        hn = rmsnorm_rows(h, norm_mix[l], BF16, tm)
        if l % 2 == 0:
            e = l // 2
            w_in = even_w_in[e]
            n_main = 6 * ia
            scale = jnp.concatenate([jnp.full((ia,), SCALE_A, F32), jnp.ones((2 * ia,), F32),
                                     jnp.full((ia,), SCALE_B, F32), jnp.ones((2 * ia,), F32)])
            pf, pb = matmul_rows(hn[None], w_in[:, :n_main].astype(BF16)[None], 0, n_main, [F32, BF16],
                                 scale=scale, tm=tm, tn=1024)
            w_f = jnp.pad(w_in[:, n_main:], ((0, 0), (0, HD - H_B))).astype(BF16)
            (fl,) = matmul_rows(hn[None], w_f[None], 0, HD, [F32], tm=tm, tn=HD)
            logf = jax.nn.log_sigmoid(fl[:, :H_B] + even_b_f[e])
            lam_init = 0.8 - 0.6 * math.exp(-0.3 * l)
            lq = diff_lambda[e].astype(F32)
            lam = jnp.exp(jnp.sum(lq[0] * lq[1])) - jnp.exp(jnp.sum(lq[2] * lq[3])) + lam_init
            slopes = jnp.asarray(_alibi_slopes(H_A))
            logf_p = logf[:mp].reshape(batch, seq, H_B)
            sfx = suffix_sum_rows(jnp.transpose(logf_p, (0, 2, 1)).reshape(batch * H_B, seq))
            o_p = even_attention_prompt(pb, sfx.reshape(batch, H_B, seq), slopes, lam, diff_subln[e], lam_init,
                                        batch=batch, seq=seq, tq=256)
            pf_s = pf[mp:mp + ms].reshape(bs, ts, 6, H_A, HD)
            logf_s = logf[mp:mp + ms].reshape(bs, ts, H_B)
            oa, ob = _even_decode_attend(pf_s[:, :, 0], pf_s[:, :, 3], pf_s[:, :, 1], pf_s[:, :, 2], pf_s[:, :, 4],
                                         pf_s[:, :, 5], logf_s, gather(cache_a_kv, e), gather(cache_b_kv, e),
                                         gather(cache_b_logf, e), lam, slopes)
            oa = _rmsnorm_jnp(oa, diff_subln[e]) * (1.0 - lam_init)
            o_s = jnp.stack([oa.reshape(ms, ia), ob.reshape(ms, ia)]).astype(BF16)
            o_all = jnp.concatenate([o_p, o_s, jnp.zeros((2, m_pad - mp - ms, ia), BF16)], axis=1)
            (h,) = matmul_rows(o_all, even_w_out[e].astype(BF16).reshape(2, ia, d), 0, d, [F32],
                               res=h, tm=tm, tn=1024)
            outs["a_p"].append(pf[:mp, ia:3 * ia].reshape(batch, seq, 2, H_A, HD))
            outs["b_p"].append(pf[:mp, 4 * ia:6 * ia].reshape(batch, seq, 2, H_B, HD))
            outs["lf_p"].append(logf_p)
            outs["a_s"].append(pf[mp:mp + ms, ia:3 * ia].reshape(bs, ts, 2, H_A, HD))
            outs["b_s"].append(pf[mp:mp + ms, 4 * ia:6 * ia].reshape(bs, ts, 2, H_B, HD))
            outs["lf_s"].append(logf_s)
        else:
            o = l // 2
            w_in = odd_w_in[o]
            nq_, nkv, nw = H_C * HD, 4 * G_C * HD, 2 * G_C * HD
            n_main = nq_ + nkv + nw
            scale = jnp.concatenate([jnp.full((nq_,), SCALE_B, F32), jnp.ones((nkv + nw,), F32)])
            pf, pb = matmul_rows(hn[None], w_in[:, :n_main].astype(BF16)[None], 0, n_main, [F32, BF16],
                                 scale=scale, tm=tm, tn=1024)
            w_g = jnp.pad(w_in[:, n_main:], ((0, 0), (0, HD - 3 * H_C))).astype(BF16)
            (gl,) = matmul_rows(hn[None], w_g[None], 0, HD, [F32], tm=tm, tn=HD)
            gates = jax.nn.sigmoid(gl[:, :3 * H_C] + odd_b_gate[o])
            slopes = _alibi_slopes(H_C)
            n_chunk = seq // S_CMP
            n_sel = -(-seq // L_SEL)
            w1 = cmp_w1[o]
            w1cat = jnp.concatenate([w1[:, :S_CMP], w1[:, S_CMP:]], axis=-1).astype(BF16)
            ckv = compress_blocks(pf, nq_ // HD, w1cat, w1.reshape(2, L_CMP * HD, HD).astype(BF16),
                                  cmp_pe[o].reshape(2, 1, L_CMP * HD), cmp_w2[o].astype(BF16),
                                  cmp_b2[o].reshape(2, 1, HD), batch=batch, seq=seq)
            cvt = jnp.transpose(ckv[:, 1], (0, 1, 3, 2))
            c_sv = nq_ + 3 * G_C * HD
            c_wv = nq_ + nkv + G_C * HD
            svt = _head_major_t(pb[:mp, c_sv:c_sv + G_C * HD], batch, seq, SEL_TK)
            wvt = _head_major_t(pb[:mp, c_wv:c_wv + G_C * HD], batch, seq, WIN_TK)
            nqb = seq // NSA_TQ
            gates_t = jnp.transpose(gates[:mp].reshape(batch, nqb, NSA_TQ, G_C, R_C, 3), (0, 3, 1, 5, 4, 2))
            gates_t = gates_t.reshape(batch, G_C, nqb, 3, NSA_LANES)
            slope_t = jnp.asarray(np.repeat(slopes.reshape(G_C, R_C, 1), NSA_TQ, axis=2).reshape(G_C, 1, NSA_LANES))
            o_p = nsa_attention_prompt(pb, pb, (nq_ + 2 * G_C * HD) // HD, (nq_ + nkv) // HD, ckv, cvt,
                                       _sel_overlap_t(n_chunk, n_sel), svt, wvt, gates_t, slope_t,
                                       batch=batch, seq=seq)
            kv4_p = pf[:mp, nq_:nq_ + nkv].reshape(batch, seq, 4, G_C, HD)
            wkv_p = pf[:mp, nq_ + nkv:n_main].reshape(batch, seq, 2, G_C, HD)
            pf_s = pf[mp:mp + ms]
            kv4_s = pf_s[:, nq_:nq_ + nkv].reshape(bs, ts, 4, G_C, HD)
            wkv_s = pf_s[:, nq_ + nkv:n_main].reshape(bs, ts, 2, G_C, HD)
            o_s, win_s = _nsa_decode_attend(pf_s[:, :nq_].reshape(bs, ts, H_C, HD),
                                            gates[mp:mp + ms].reshape(bs, ts, H_C, 3), kv4_s, wkv_s,
                                            gather(cache_nsa_kv, o), state_win_kv[o], cmp_pe[o], cmp_w1[o],
                                            cmp_w2[o], cmp_b2[o], jnp.asarray(slopes))
            o_all = jnp.concatenate([o_p, o_s.reshape(ms, d).astype(BF16),
                                     jnp.zeros((m_pad - mp - ms, d), BF16)], axis=0)
            (h,) = matmul_rows(o_all[None], odd_w_out[o].astype(BF16)[None], 0, d, [F32], res=h, tm=tm, tn=1024)
            outs["n_p"].append(kv4_p)
            outs["w_p"].append(wkv_p[:, -min(WINDOW, seq):])
            outs["n_s"].append(kv4_s)
            outs["w_s"].append(win_s)
        hn = rmsnorm_rows(h, norm_ffn[l], BF16, tm)
        h = ffn_rows(hn, ffn_w1[l].astype(BF16), ffn_w3[l].astype(BF16), ffn_w2[l].astype(BF16), h, tm=tm, tf=512)
    y = rmsnorm_rows(h, norm_final, F32, tm)
    return (y[:mp].reshape(batch, seq, d), y[mp:mp + ms].reshape(bs, ts, d),
            jnp.stack(outs["a_p"]), jnp.stack(outs["b_p"]), jnp.stack(outs["lf_p"]), jnp.stack(outs["n_p"]),
            jnp.stack(outs["w_p"]), jnp.stack(outs["a_s"]), jnp.stack(outs["b_s"]), jnp.stack(outs["lf_s"]),
            jnp.stack(outs["n_s"]), jnp.stack(outs["w_s"]))
```
